```python
import jax, jax.numpy as jnp
from jax import lax
import numpy as np

D_MODEL = 1024
BATCH = 8
SEQ = 4096
DEPTH = 1

D_MIX = D_MODEL
D_A = D_MIX // 2
D_B = D_MIX - D_A
N_HEADS_A = 4
HEAD_DIM_A = D_A // N_HEADS_A
CHUNK = 128
POOL_WINDOWS = (2, 4, 8, 16)
N_POOL_GROUPS = len(POOL_WINDOWS)
POOL_GROUP_DIM = D_B // N_POOL_GROUPS
D_FF = 4 * D_MODEL
N_MOD = 6
EPS = 1e-6

kernel_name = "hybrid_gmlp_pool_sqrelu_block"


def rms_norm(x, g):
    xf = x.astype(jnp.float32)
    y = xf * lax.rsqrt(jnp.mean(xf * xf, axis=-1, keepdims=True) + EPS)
    return (y * g.astype(jnp.float32)).astype(x.dtype)


def layer_norm(x, g, b):
    xf = x.astype(jnp.float32)
    mu = jnp.mean(xf, axis=-1, keepdims=True)
    var = jnp.mean(jnp.square(xf - mu), axis=-1, keepdims=True)
    y = (xf - mu) * lax.rsqrt(var + EPS)
    return (y * g.astype(jnp.float32) + b.astype(jnp.float32)).astype(x.dtype)


def spatial_gating(z_a, w_spatial, b_spatial, ln_v_gain, ln_v_bias):
    b, s, _ = z_a.shape
    z_a = jax.nn.gelu(z_a)
    u, v = z_a[..., :D_A], z_a[..., D_A:]
    v = layer_norm(v, ln_v_gain, ln_v_bias)
    v = v.reshape(b, s // CHUNK, CHUNK, N_HEADS_A, HEAD_DIM_A)
    mask = jnp.tril(jnp.ones((CHUNK, CHUNK), dtype=w_spatial.dtype))
    w_causal = w_spatial * mask[None]
    mixed = jnp.einsum("hts,bnshd->bnthd", w_causal, v)
    mixed = mixed + b_spatial.T[:, :, None]
    return u * mixed.reshape(b, s, D_A)


def multiscale_pool(z_b, w_pool, b_pool, pool_scale):
    b, s, _ = z_b.shape
    zg = z_b.reshape(b, s, N_POOL_GROUPS, POOL_GROUP_DIM)
    zf = zg.astype(jnp.float32)
    cs = jnp.concatenate(
        [jnp.zeros((b, 1, N_POOL_GROUPS, POOL_GROUP_DIM), jnp.float32), jnp.cumsum(zf, axis=1)],
        axis=1)
    pos = jnp.arange(s, dtype=jnp.float32)
    pooled = []
    for g, w in enumerate(POOL_WINDOWS):
        csg = cs[:, :, g]
        lower = jnp.concatenate(
            [jnp.zeros((b, w - 1, POOL_GROUP_DIM), jnp.float32), csg[:, : s + 1 - w]], axis=1)
        count = jnp.minimum(pos + 1.0, float(w))[None, :, None]
        pooled.append((csg[:, 1:] - lower) / count)
    pooled = jnp.stack(pooled, axis=2)
    diff = (pooled - zf).astype(z_b.dtype)
    y = jnp.einsum("bsgc,gcd->bsgd", diff, w_pool) + b_pool
    return y.reshape(b, s, D_B) * pool_scale


def setup_inputs(seed: int = 0) -> dict:
    key = jax.random.key(seed)
    ks = jax.random.split(key, 20)
    f32 = jnp.float32
    nrm = lambda k, shape, scale: jax.random.normal(k, shape, f32) * scale
    return {
        "x": nrm(ks[0], (BATCH, SEQ, D_MODEL), 1.0),
        "c": nrm(ks[1], (BATCH, D_MODEL), 1.0),
        "w_ada": nrm(ks[2], (D_MODEL, N_MOD * D_MODEL), 0.5 * D_MODEL ** -0.5),
        "b_ada": nrm(ks[3], (N_MOD * D_MODEL,), 0.01),
        "norm1_pre": 1.0 + nrm(ks[4], (D_MODEL,), 0.02),
        "norm1_post": 1.0 + nrm(ks[5], (D_MODEL,), 0.02),
        "w_in": nrm(ks[6], (D_MODEL, 2 * D_A + D_B), D_MODEL ** -0.5),
        "w_spatial": nrm(ks[7], (N_HEADS_A, CHUNK, CHUNK), 0.5 * CHUNK ** -0.5),
        "b_spatial": 1.0 + nrm(ks[8], (N_HEADS_A, CHUNK), 0.02),
        "ln_v_gain": 1.0 + nrm(ks[9], (D_A,), 0.02),
        "ln_v_bias": nrm(ks[10], (D_A,), 0.02),
        "w_pool": nrm(ks[11], (N_POOL_GROUPS, POOL_GROUP_DIM, POOL_GROUP_DIM), POOL_GROUP_DIM ** -0.5),
        "b_pool": nrm(ks[12], (N_POOL_GROUPS, POOL_GROUP_DIM), 0.02),
        "pool_scale": 1.0 + nrm(ks[13], (D_B,), 0.02),
        "w_out": nrm(ks[14], (D_MIX, D_MODEL), D_MIX ** -0.5),
        "norm2_pre": 1.0 + nrm(ks[15], (D_MODEL,), 0.02),
        "norm2_post": 1.0 + nrm(ks[16], (D_MODEL,), 0.02),
        "w_fc1": nrm(ks[17], (DEPTH, D_MODEL, D_FF), D_MODEL ** -0.5)[0],
        "w_fc2": nrm(ks[18], (D_FF, D_MODEL), D_FF ** -0.5),
    }


def reference(x, c, w_ada, b_ada, norm1_pre, norm1_post, w_in, w_spatial, b_spatial,
              ln_v_gain, ln_v_bias, w_pool, b_pool, pool_scale, w_out,
              norm2_pre, norm2_post, w_fc1, w_fc2):
    mod = jax.nn.silu(c) @ w_ada + b_ada
    shift1, scale1, gate1, shift2, scale2, gate2 = [
        m[:, None, :] for m in jnp.split(mod, N_MOD, axis=-1)]

    for _ in range(DEPTH):
        h = rms_norm(x, norm1_pre) * (1.0 + scale1) + shift1
        z = h @ w_in
        y_a = spatial_gating(z[..., : 2 * D_A], w_spatial, b_spatial, ln_v_gain, ln_v_bias)
        y_b = multiscale_pool(z[..., 2 * D_A:], w_pool, b_pool, pool_scale)
        mix = jnp.concatenate([y_a, y_b], axis=-1) @ w_out
        x = x + gate1 * rms_norm(mix, norm1_post)

        h = rms_norm(x, norm2_pre) * (1.0 + scale2) + shift2
        f = jnp.square(jax.nn.relu(h @ w_fc1)) @ w_fc2
        x = x + gate2 * rms_norm(f, norm2_post)
    return x
```

```python
import functools

import jax
import jax.numpy as jnp
from jax import lax
from jax.experimental import pallas as pl
from jax.experimental.pallas import tpu as pltpu

D_MODEL = 1024
D_A = 512
D_B = 512
N_HEADS_A = 4
HEAD_DIM_A = D_A // N_HEADS_A
CHUNK = 128
POOL_WINDOWS = (2, 4, 8, 16)
POOL_GROUP_DIM = D_B // len(POOL_WINDOWS)
D_FF = 4 * D_MODEL
N_MOD = 6
EPS = 1e-6

HALO = 16
PAD = 8

TS = 512
TM = 512
FC = 512
ADA_BN = 1536

VMEM_LIMIT_BYTES = 56 * 1024 * 1024


def _ada_kernel(c_ref, w_ref, b_ref, o_ref):
    c = c_ref[...]
    s = c * jax.nn.sigmoid(c)
    o_ref[...] = jnp.dot(s.astype(jnp.bfloat16), w_ref[...].astype(jnp.bfloat16),
                         preferred_element_type=jnp.float32) + b_ref[...]


def _rms_scale(v):
    return lax.rsqrt(jnp.mean(v * v, axis=-1, keepdims=True) + EPS)


def _mixer_kernel(x_ref, mod_ref, g_pre_ref, g_post_ref, w_in_ref, wsp_ref, bsp_ref,
                  lng_ref, lnb_ref, wpool_ref, bpool_ref, pscale_ref, w_out_ref,
                  o_ref, p_scr, q_scr, y_scr):
    s_idx = pl.program_id(1)
    ts = x_ref.shape[1]
    n_chunks = ts // CHUNK
    rows = HALO + ts
    cur = PAD + HALO

    shift1 = mod_ref[0, 0:1, :]
    scale1 = mod_ref[0, 1:2, :]
    gate1 = mod_ref[0, 2:3, :]

    x = x_ref[0]
    a = g_pre_ref[...] * (1.0 + scale1)
    h = (x * _rms_scale(x)) * a + shift1
    z = jnp.dot(h.astype(jnp.bfloat16), w_in_ref[...], preferred_element_type=jnp.float32)

    za = jax.nn.gelu(z[:, :2 * D_A])
    u = za[:, :D_A]
    v = za[:, D_A:]
    mu = jnp.mean(v, axis=-1, keepdims=True)
    vc = v - mu
    var = jnp.mean(vc * vc, axis=-1, keepdims=True)
    vn = ((vc * lax.rsqrt(var + EPS)) * lng_ref[...] + lnb_ref[...]).astype(jnp.bfloat16)

    t_i = lax.broadcasted_iota(jnp.int32, (CHUNK, CHUNK), 0)
    s_i = lax.broadcasted_iota(jnp.int32, (CHUNK, CHUNK), 1)
    causal = s_i <= t_i
    for hd in range(N_HEADS_A):
        c0 = hd * HEAD_DIM_A
        w_c = jnp.where(causal, wsp_ref[hd], 0.0).astype(jnp.bfloat16)
        rhs = jnp.concatenate(
            [vn[c * CHUNK:(c + 1) * CHUNK, c0:c0 + HEAD_DIM_A] for c in range(n_chunks)], axis=1)
        mixed = jnp.dot(w_c, rhs, preferred_element_type=jnp.float32)
        bias = bsp_ref[hd]
        for c in range(n_chunks):
            m_c = mixed[:, c * HEAD_DIM_A:(c + 1) * HEAD_DIM_A] + bias
            y_scr[c * CHUNK:(c + 1) * CHUNK, c0:c0 + HEAD_DIM_A] = (
                u[c * CHUNK:(c + 1) * CHUNK, c0:c0 + HEAD_DIM_A] * m_c).astype(jnp.bfloat16)

    zb = z[:, 2 * D_A:]
    p_scr[0:PAD, :] = jnp.zeros((PAD, D_B), jnp.float32)
    q_scr[0:PAD, :] = jnp.zeros((PAD, D_B), jnp.float32)

    @pl.when(s_idx == 0)
    def _():
        p_scr[PAD:cur, :] = jnp.zeros((HALO, D_B), jnp.float32)

    @pl.when(s_idx != 0)
    def _():
        p_scr[PAD:cur, :] = p_scr[PAD + ts:cur + ts, :]

    p_scr[cur:cur + ts, :] = zb
    src, dst = p_scr, q_scr
    shift = 1
    for level in range(len(POOL_WINDOWS)):
        lo = level * POOL_GROUP_DIM
        dst[PAD:PAD + rows, lo:] = (src[PAD:PAD + rows, lo:]
                                    + src[PAD - shift:PAD - shift + rows, lo:])
        src, dst = dst, src
        shift *= 2
    finals = [q_scr, p_scr, q_scr, p_scr]

    pos = lax.broadcasted_iota(jnp.int32, (ts, POOL_GROUP_DIM), 0) + s_idx * ts
    posf = (pos + 1).astype(jnp.float32)
    for g, w in enumerate(POOL_WINDOWS):
        lo = g * POOL_GROUP_DIM
        wsum = finals[g][cur:cur + ts, lo:lo + POOL_GROUP_DIM]
        inv_count = 1.0 / jnp.minimum(posf, float(w))
        diff = wsum * inv_count - zb[:, lo:lo + POOL_GROUP_DIM]
        yb = jnp.dot(diff.astype(jnp.bfloat16), wpool_ref[g], preferred_element_type=jnp.float32)
        yb = (yb + bpool_ref[g:g + 1, :]) * pscale_ref[:, lo:lo + POOL_GROUP_DIM]
        y_scr[:, D_A + lo:D_A + lo + POOL_GROUP_DIM] = yb.astype(jnp.bfloat16)

    p_scr[cur:cur + ts, :] = zb

    mix = jnp.dot(y_scr[...], w_out_ref[...], preferred_element_type=jnp.float32)
    o_ref[0] = x + gate1 * ((mix * _rms_scale(mix)) * g_post_ref[...])


def _mlp_kernel(x_ref, mod_ref, g_pre_ref, g_post_ref, w1_ref, w2_ref, o_ref, h_scr, acc_scr):
    shift2 = mod_ref[0, 3:4, :]
    scale2 = mod_ref[0, 4:5, :]
    gate2 = mod_ref[0, 5:6, :]

    x = x_ref[0]
    a = g_pre_ref[...] * (1.0 + scale2)
    h_scr[...] = ((x * _rms_scale(x)) * a + shift2).astype(jnp.bfloat16)
    acc_scr[...] = jnp.zeros_like(acc_scr)

    def body(j, carry):
        col = pl.multiple_of(j * FC, FC)
        t = jnp.dot(h_scr[...], w1_ref[:, pl.ds(col, FC)], preferred_element_type=jnp.float32)
        t = jnp.maximum(t, 0.0)
        p = (t * t).astype(jnp.bfloat16)
        acc_scr[...] += jnp.dot(p, w2_ref[pl.ds(col, FC), :], preferred_element_type=jnp.float32)
        return carry

    lax.fori_loop(0, D_FF // FC, body, 0)
    f = acc_scr[...]
    o_ref[0] = x + gate2 * ((f * _rms_scale(f)) * g_post_ref[...])


def _const_spec(shape):
    nd = len(shape)
    return pl.BlockSpec(shape, lambda *_: (0,) * nd, pipeline_mode=pl.Buffered(1))


def kernel(x, c, w_ada, b_ada, norm1_pre, norm1_post, w_in, w_spatial, b_spatial, ln_v_gain,
           ln_v_bias, w_pool, b_pool, pool_scale, w_out, norm2_pre, norm2_post, w_fc1, w_fc2):
    b, s, d = x.shape
    bf16 = jnp.bfloat16
    row = lambda v: v.reshape(1, -1)

    mod = pl.pallas_call(
        _ada_kernel,
        grid=(N_MOD * d // ADA_BN,),
        in_specs=[pl.BlockSpec((b, d), lambda j: (0, 0)),
                  pl.BlockSpec((d, ADA_BN), lambda j: (0, j)),
                  pl.BlockSpec((1, ADA_BN), lambda j: (0, j))],
        out_specs=pl.BlockSpec((b, ADA_BN), lambda j: (0, j)),
        out_shape=jax.ShapeDtypeStruct((b, N_MOD * d), jnp.float32),
        name="adaln_mod",
    )(c, w_ada, row(b_ada))
    mod = mod.reshape(b, N_MOD, d)

    bsp = jnp.broadcast_to(b_spatial[:, :, None], (N_HEADS_A, CHUNK, HEAD_DIM_A))

    x1 = pl.pallas_call(
        _mixer_kernel,
        grid=(b, s // TS),
        in_specs=[pl.BlockSpec((1, TS, d), lambda i, j: (i, j, 0)),
                  pl.BlockSpec((1, N_MOD, d), lambda i, j: (i, 0, 0)),
                  _const_spec((1, d)), _const_spec((1, d)),
                  _const_spec(w_in.shape), _const_spec(w_spatial.shape), _const_spec(bsp.shape),
                  _const_spec((1, D_A)), _const_spec((1, D_A)),
                  _const_spec(w_pool.shape), _const_spec(b_pool.shape), _const_spec((1, D_B)),
                  _const_spec(w_out.shape)],
        out_specs=pl.BlockSpec((1, TS, d), lambda i, j: (i, j, 0)),
        out_shape=jax.ShapeDtypeStruct(x.shape, x.dtype),
        scratch_shapes=[pltpu.VMEM((PAD + HALO + TS, D_B), jnp.float32),
                        pltpu.VMEM((PAD + HALO + TS, D_B), jnp.float32),
                        pltpu.VMEM((TS, D_A + D_B), bf16)],
        compiler_params=pltpu.CompilerParams(
            dimension_semantics=("arbitrary", "arbitrary"), vmem_limit_bytes=VMEM_LIMIT_BYTES),
        name="token_mixer",
    )(x, mod, row(norm1_pre), row(norm1_post), w_in.astype(bf16), w_spatial, bsp,
      row(ln_v_gain), row(ln_v_bias), w_pool.astype(bf16), b_pool, row(pool_scale),
      w_out.astype(bf16))

    out = pl.pallas_call(
        _mlp_kernel,
        grid=(b, s // TM),
        in_specs=[pl.BlockSpec((1, TM, d), lambda i, j: (i, j, 0)),
                  pl.BlockSpec((1, N_MOD, d), lambda i, j: (i, 0, 0)),
                  _const_spec((1, d)), _const_spec((1, d)),
                  _const_spec(w_fc1.shape), _const_spec(w_fc2.shape)],
        out_specs=pl.BlockSpec((1, TM, d), lambda i, j: (i, j, 0)),
        out_shape=jax.ShapeDtypeStruct(x.shape, x.dtype),
        scratch_shapes=[pltpu.VMEM((TM, d), bf16), pltpu.VMEM((TM, d), jnp.float32)],
        compiler_params=pltpu.CompilerParams(
            dimension_semantics=("arbitrary", "arbitrary"), vmem_limit_bytes=VMEM_LIMIT_BYTES),
        name="sqrelu_mlp",
    )(x1, mod, row(norm2_pre), row(norm2_post), w_fc1.astype(bf16), w_fc2.astype(bf16))
    return out
```

```python
import functools

import jax
import jax.numpy as jnp
from jax import lax
from jax.experimental import pallas as pl
from jax.experimental.pallas import tpu as pltpu

D_MODEL = 1024
D_A = 512
D_B = 512
N_HEADS_A = 4
HEAD_DIM_A = D_A // N_HEADS_A
CHUNK = 128
POOL_WINDOWS = (2, 4, 8, 16)
POOL_GROUP_DIM = D_B // len(POOL_WINDOWS)
D_FF = 4 * D_MODEL
N_MOD = 6
EPS = 1e-6

HALO = 16
PAD = 8

TS = 512
TM = 512
FC = 512
ADA_BN = 1536

VMEM_LIMIT_BYTES = 56 * 1024 * 1024


def _ada_kernel(c_ref, w_ref, b_ref, o_ref):
    c = c_ref[...]
    s = c * jax.nn.sigmoid(c)
    o_ref[...] = jnp.dot(s.astype(jnp.bfloat16), w_ref[...].astype(jnp.bfloat16),
                         preferred_element_type=jnp.float32) + b_ref[...]


def _rms_scale(v):
    return lax.rsqrt(jnp.mean(v * v, axis=-1, keepdims=True) + EPS)


def _mixer_kernel(x_ref, mod_ref, g_pre_ref, g_post_ref, w_in_ref, wsp_ref, bsp_ref,
                  lng_ref, lnb_ref, wpool_ref, bpool_ref, pscale_ref, w_out_ref,
                  o_ref, p_scr, q_scr, y_scr):
    s_idx = pl.program_id(1)
    ts = x_ref.shape[1]
    n_chunks = ts // CHUNK
    rows = HALO + ts
    cur = PAD + HALO

    shift1 = mod_ref[0, 0:1, :]
    scale1 = mod_ref[0, 1:2, :]
    gate1 = mod_ref[0, 2:3, :]

    x = x_ref[0]
    a = g_pre_ref[...] * (1.0 + scale1)
    h = (x * _rms_scale(x)) * a + shift1
    z = jnp.dot(h.astype(jnp.bfloat16), w_in_ref[...], preferred_element_type=jnp.float32)

    za = jax.nn.gelu(z[:, :2 * D_A])
    u = za[:, :D_A]
    v = za[:, D_A:]
    mu = jnp.mean(v, axis=-1, keepdims=True)
    vc = v - mu
    var = jnp.mean(vc * vc, axis=-1, keepdims=True)
    vn = ((vc * lax.rsqrt(var + EPS)) * lng_ref[...] + lnb_ref[...]).astype(jnp.bfloat16)

    t_i = lax.broadcasted_iota(jnp.int32, (CHUNK, CHUNK), 0)
    s_i = lax.broadcasted_iota(jnp.int32, (CHUNK, CHUNK), 1)
    causal = s_i <= t_i
    for hd in range(N_HEADS_A):
        c0 = hd * HEAD_DIM_A
        w_c = jnp.where(causal, wsp_ref[hd], 0.0).astype(jnp.bfloat16)
        rhs = jnp.concatenate(
            [vn[c * CHUNK:(c + 1) * CHUNK, c0:c0 + HEAD_DIM_A] for c in range(n_chunks)], axis=1)
        mixed = jnp.dot(w_c, rhs, preferred_element_type=jnp.float32)
        bias = bsp_ref[hd]
        for c in range(n_chunks):
            m_c = mixed[:, c * HEAD_DIM_A:(c + 1) * HEAD_DIM_A] + bias
            y_scr[c * CHUNK:(c + 1) * CHUNK, c0:c0 + HEAD_DIM_A] = (
                u[c * CHUNK:(c + 1) * CHUNK, c0:c0 + HEAD_DIM_A] * m_c).astype(jnp.bfloat16)

    zb = z[:, 2 * D_A:]
    p_scr[0:PAD, :] = jnp.zeros((PAD, D_B), jnp.float32)
    q_scr[0:PAD, :] = jnp.zeros((PAD, D_B), jnp.float32)

    @pl.when(s_idx == 0)
    def _():
        p_scr[PAD:cur, :] = jnp.zeros((HALO, D_B), jnp.float32)

    @pl.when(s_idx != 0)
    def _():
        p_scr[PAD:cur, :] = p_scr[PAD + ts:cur + ts, :]

    p_scr[cur:cur + ts, :] = zb
    src, dst = p_scr, q_scr
    shift = 1
    for level in range(len(POOL_WINDOWS)):
        lo = level * POOL_GROUP_DIM
        dst[PAD:PAD + rows, lo:] = (src[PAD:PAD + rows, lo:]
                                    + src[PAD - shift:PAD - shift + rows, lo:])
        src, dst = dst, src
        shift *= 2
    finals = [q_scr, p_scr, q_scr, p_scr]

    pos = lax.broadcasted_iota(jnp.int32, (ts, POOL_GROUP_DIM), 0) + s_idx * ts
    posf = (pos + 1).astype(jnp.float32)
    for g, w in enumerate(POOL_WINDOWS):
        lo = g * POOL_GROUP_DIM
        wsum = finals[g][cur:cur + ts, lo:lo + POOL_GROUP_DIM]
        inv_count = 1.0 / jnp.minimum(posf, float(w))
        diff = wsum * inv_count - zb[:, lo:lo + POOL_GROUP_DIM]
        yb = jnp.dot(diff.astype(jnp.bfloat16), wpool_ref[g], preferred_element_type=jnp.float32)
        yb = (yb + bpool_ref[g:g + 1, :]) * pscale_ref[:, lo:lo + POOL_GROUP_DIM]
        y_scr[:, D_A + lo:D_A + lo + POOL_GROUP_DIM] = yb.astype(jnp.bfloat16)

    p_scr[cur:cur + ts, :] = zb

    mix = jnp.dot(y_scr[...], w_out_ref[...], preferred_element_type=jnp.float32)
    o_ref[0] = x + gate1 * ((mix * _rms_scale(mix)) * g_post_ref[...])


def _mlp_kernel(x_ref, mod_ref, g_pre_ref, g_post_ref, w1_ref, w2_ref, o_ref):
    shift2 = mod_ref[0, 3:4, :]
    scale2 = mod_ref[0, 4:5, :]
    gate2 = mod_ref[0, 5:6, :]

    x = x_ref[0]
    a = g_pre_ref[...] * (1.0 + scale2)
    h = ((x * _rms_scale(x)) * a + shift2).astype(jnp.bfloat16)
    t = jnp.maximum(jnp.dot(h, w1_ref[...], preferred_element_type=jnp.float32), 0.0)
    p = (t * t).astype(jnp.bfloat16)
    f = jnp.dot(p, w2_ref[...], preferred_element_type=jnp.float32)
    o_ref[0] = x + (gate2 * g_post_ref[...]) * (f * _rms_scale(f))


def _const_spec(shape):
    nd = len(shape)
    return pl.BlockSpec(shape, lambda *_: (0,) * nd, pipeline_mode=pl.Buffered(1))


def kernel(x, c, w_ada, b_ada, norm1_pre, norm1_post, w_in, w_spatial, b_spatial, ln_v_gain,
           ln_v_bias, w_pool, b_pool, pool_scale, w_out, norm2_pre, norm2_post, w_fc1, w_fc2):
    b, s, d = x.shape
    bf16 = jnp.bfloat16
    row = lambda v: v.reshape(1, -1)

    mod = pl.pallas_call(
        _ada_kernel,
        grid=(N_MOD * d // ADA_BN,),
        in_specs=[pl.BlockSpec((b, d), lambda j: (0, 0)),
                  pl.BlockSpec((d, ADA_BN), lambda j: (0, j)),
                  pl.BlockSpec((1, ADA_BN), lambda j: (0, j))],
        out_specs=pl.BlockSpec((b, ADA_BN), lambda j: (0, j)),
        out_shape=jax.ShapeDtypeStruct((b, N_MOD * d), jnp.float32),
        name="adaln_mod",
    )(c, w_ada, row(b_ada))
    mod = mod.reshape(b, N_MOD, d)

    bsp = jnp.broadcast_to(b_spatial[:, :, None], (N_HEADS_A, CHUNK, HEAD_DIM_A))

    x1 = pl.pallas_call(
        _mixer_kernel,
        grid=(b, s // TS),
        in_specs=[pl.BlockSpec((1, TS, d), lambda i, j: (i, j, 0)),
                  pl.BlockSpec((1, N_MOD, d), lambda i, j: (i, 0, 0)),
                  _const_spec((1, d)), _const_spec((1, d)),
                  _const_spec(w_in.shape), _const_spec(w_spatial.shape), _const_spec(bsp.shape),
                  _const_spec((1, D_A)), _const_spec((1, D_A)),
                  _const_spec(w_pool.shape), _const_spec(b_pool.shape), _const_spec((1, D_B)),
                  _const_spec(w_out.shape)],
        out_specs=pl.BlockSpec((1, TS, d), lambda i, j: (i, j, 0)),
        out_shape=jax.ShapeDtypeStruct(x.shape, x.dtype),
        scratch_shapes=[pltpu.VMEM((PAD + HALO + TS, D_B), jnp.float32),
                        pltpu.VMEM((PAD + HALO + TS, D_B), jnp.float32),
                        pltpu.VMEM((TS, D_A + D_B), bf16)],
        compiler_params=pltpu.CompilerParams(
            dimension_semantics=("arbitrary", "arbitrary"), vmem_limit_bytes=VMEM_LIMIT_BYTES),
        name="token_mixer",
    )(x, mod, row(norm1_pre), row(norm1_post), w_in.astype(bf16), w_spatial, bsp,
      row(ln_v_gain), row(ln_v_bias), w_pool.astype(bf16), b_pool, row(pool_scale),
      w_out.astype(bf16))

    out = pl.pallas_call(
        _mlp_kernel,
        grid=(b, s // TM),
        in_specs=[pl.BlockSpec((1, TM, d), lambda i, j: (i, j, 0)),
                  pl.BlockSpec((1, N_MOD, d), lambda i, j: (i, 0, 0)),
                  _const_spec((1, d)), _const_spec((1, d)),
                  _const_spec(w_fc1.shape), _const_spec(w_fc2.shape)],
        out_specs=pl.BlockSpec((1, TM, d), lambda i, j: (i, j, 0)),
        out_shape=jax.ShapeDtypeStruct(x.shape, x.dtype),
        compiler_params=pltpu.CompilerParams(
            dimension_semantics=("arbitrary", "arbitrary"), vmem_limit_bytes=VMEM_LIMIT_BYTES),
        name="sqrelu_mlp",
    )(x1, mod, row(norm2_pre), row(norm2_post), w_fc1.astype(bf16), w_fc2.astype(bf16))
    return out
```

```python
import math

import jax
import jax.numpy as jnp
from jax import lax
from jax.experimental import pallas as pl
from jax.experimental.pallas import tpu as pltpu

D_MODEL = 1024
D_A = 512
D_B = 512
N_HEADS_A = 4
HEAD_DIM_A = D_A // N_HEADS_A
CHUNK = 128
POOL_WINDOWS = (2, 4, 8, 16)
POOL_GROUP_DIM = D_B // len(POOL_WINDOWS)
D_FF = 4 * D_MODEL
N_MOD = 6
EPS = 1e-6

GELU_A = -2.0 * math.sqrt(2.0 / math.pi) * math.log2(math.e)
GELU_B = GELU_A * 0.044715

HALO = 16
PAD = 8

SLAB = 512
N_SLABS = 2
TS = SLAB * N_SLABS
TM = 1024
ADA_BN = 1536

VMEM_LIMIT_BYTES = 56 * 1024 * 1024


def _ada_kernel(c_ref, w_ref, b_ref, o_ref):
    c = c_ref[...]
    s = c * jax.nn.sigmoid(c)
    o_ref[...] = jnp.dot(s.astype(jnp.bfloat16), w_ref[...].astype(jnp.bfloat16),
                         preferred_element_type=jnp.float32) + b_ref[...]


def _rms_scale(v):
    return lax.rsqrt(jnp.mean(v * v, axis=-1, keepdims=True) + EPS)


def _gelu_tanh(x):
    return x / (1.0 + jnp.exp2(x * (GELU_A + GELU_B * (x * x))))


def _pool_diffs(zb, halo, p_scr, q_scr, pos0):
    n = zb.shape[0]
    rows = HALO + n
    cur = PAD + HALO
    p_scr[0:PAD, :] = jnp.zeros((PAD, D_B), jnp.float32)
    q_scr[0:PAD, :] = jnp.zeros((PAD, D_B), jnp.float32)
    p_scr[PAD:cur, :] = halo
    p_scr[cur:cur + n, :] = zb
    src, dst = p_scr, q_scr
    shift = 1
    for level in range(len(POOL_WINDOWS)):
        lo = level * POOL_GROUP_DIM
        dst[PAD:PAD + rows, lo:] = (src[PAD:PAD + rows, lo:]
                                    + src[PAD - shift:PAD - shift + rows, lo:])
        src, dst = dst, src
        shift *= 2
    finals = [q_scr, p_scr, q_scr, p_scr]

    posf = (lax.broadcasted_iota(jnp.int32, (n, POOL_GROUP_DIM), 0) + (pos0 + 1)).astype(jnp.float32)
    diffs = []
    for g, w in enumerate(POOL_WINDOWS):
        lo = g * POOL_GROUP_DIM
        wsum = finals[g][cur:cur + n, lo:lo + POOL_GROUP_DIM]
        inv_count = 1.0 / jnp.minimum(posf, float(w))
        diffs.append(wsum * inv_count - zb[:, lo:lo + POOL_GROUP_DIM])
    return diffs


def _mixer_kernel(x_ref, mod_ref, g_pre_ref, g_post_ref, w_in_ref, wsp_ref, bsp_ref,
                  lng_ref, lnb_ref, wpool_ref, bpool_ref, pscale_ref, w_out_ref,
                  o_ref, carry_scr, pq_scr, y_scr):
    s_idx = pl.program_id(1)
    n_chunks = SLAB // CHUNK

    shift1 = mod_ref[0, 0:1, :]
    scale1 = mod_ref[0, 1:2, :]
    gate1 = mod_ref[0, 2:3, :]
    a = g_pre_ref[...] * (1.0 + scale1)
    gpost = gate1 * g_post_ref[...]

    t_i = lax.broadcasted_iota(jnp.int32, (CHUNK, CHUNK), 0)
    s_i = lax.broadcasted_iota(jnp.int32, (CHUNK, CHUNK), 1)
    causal = s_i <= t_i
    w_causal = [jnp.where(causal, wsp_ref[hd], 0.0).astype(jnp.bfloat16)
                for hd in range(N_HEADS_A)]

    @pl.when(s_idx == 0)
    def _():
        carry_scr[...] = jnp.zeros_like(carry_scr)

    halo = carry_scr[...]
    for sl in range(N_SLABS):
        r0 = sl * SLAB
        y_sl = y_scr.at[sl]
        x = x_ref[0, r0:r0 + SLAB, :]
        h = (x * _rms_scale(x)) * a + shift1
        z = jnp.dot(h.astype(jnp.bfloat16), w_in_ref[...], preferred_element_type=jnp.float32)

        za = _gelu_tanh(z[:, :2 * D_A])
        u = za[:, :D_A]
        v = za[:, D_A:]
        mu = jnp.mean(v, axis=-1, keepdims=True)
        vc = v - mu
        var = jnp.mean(vc * vc, axis=-1, keepdims=True)
        vn = ((vc * lax.rsqrt(var + EPS)) * lng_ref[...] + lnb_ref[...]).astype(jnp.bfloat16)
        for hd in range(N_HEADS_A):
            c0 = hd * HEAD_DIM_A
            rhs = jnp.concatenate(
                [vn[c * CHUNK:(c + 1) * CHUNK, c0:c0 + HEAD_DIM_A] for c in range(n_chunks)], axis=1)
            mixed = jnp.dot(w_causal[hd], rhs, preferred_element_type=jnp.float32)
            bias = bsp_ref[hd]
            for c in range(n_chunks):
                m_c = mixed[:, c * HEAD_DIM_A:(c + 1) * HEAD_DIM_A] + bias
                y_sl[c * CHUNK:(c + 1) * CHUNK, c0:c0 + HEAD_DIM_A] = (
                    u[c * CHUNK:(c + 1) * CHUNK, c0:c0 + HEAD_DIM_A] * m_c).astype(jnp.bfloat16)

        zb = z[:, 2 * D_A:]
        diffs = _pool_diffs(zb, halo, pq_scr.at[2 * sl], pq_scr.at[2 * sl + 1], s_idx * TS + r0)
        halo = zb[SLAB - HALO:, :]
        for g in range(len(POOL_WINDOWS)):
            lo = g * POOL_GROUP_DIM
            yb = jnp.dot(diffs[g].astype(jnp.bfloat16), wpool_ref[g],
                         preferred_element_type=jnp.float32)
            yb = (yb + bpool_ref[g:g + 1, :]) * pscale_ref[:, lo:lo + POOL_GROUP_DIM]
            y_sl[:, D_A + lo:D_A + lo + POOL_GROUP_DIM] = yb.astype(jnp.bfloat16)

        mix = jnp.dot(y_sl[...], w_out_ref[...], preferred_element_type=jnp.float32)
        o_ref[0, r0:r0 + SLAB, :] = x + gpost * (mix * _rms_scale(mix))
    carry_scr[...] = halo


def _mlp_kernel(x_ref, mod_ref, g_pre_ref, g_post_ref, w1_ref, w2_ref, o_ref):
    shift2 = mod_ref[0, 3:4, :]
    scale2 = mod_ref[0, 4:5, :]
    gate2 = mod_ref[0, 5:6, :]

    x = x_ref[0]
    a = g_pre_ref[...] * (1.0 + scale2)
    h = ((x * _rms_scale(x)) * a + shift2).astype(jnp.bfloat16)
    t = jnp.maximum(jnp.dot(h, w1_ref[...], preferred_element_type=jnp.float32), 0.0)
    p = (t * t).astype(jnp.bfloat16)
    f = jnp.dot(p, w2_ref[...], preferred_element_type=jnp.float32)
    o_ref[0] = x + (gate2 * g_post_ref[...]) * (f * _rms_scale(f))


def _const_spec(shape):
    nd = len(shape)
    return pl.BlockSpec(shape, lambda *_: (0,) * nd, pipeline_mode=pl.Buffered(1))


def kernel(x, c, w_ada, b_ada, norm1_pre, norm1_post, w_in, w_spatial, b_spatial, ln_v_gain,
           ln_v_bias, w_pool, b_pool, pool_scale, w_out, norm2_pre, norm2_post, w_fc1, w_fc2):
    b, s, d = x.shape
    bf16 = jnp.bfloat16
    row = lambda v: v.reshape(1, -1)

    mod = pl.pallas_call(
        _ada_kernel,
        grid=(N_MOD * d // ADA_BN,),
        in_specs=[pl.BlockSpec((b, d), lambda j: (0, 0)),
                  pl.BlockSpec((d, ADA_BN), lambda j: (0, j)),
                  pl.BlockSpec((1, ADA_BN), lambda j: (0, j))],
        out_specs=pl.BlockSpec((b, ADA_BN), lambda j: (0, j)),
        out_shape=jax.ShapeDtypeStruct((b, N_MOD * d), jnp.float32),
        name="adaln_mod",
    )(c, w_ada, row(b_ada))
    mod = mod.reshape(b, N_MOD, d)

    bsp = jnp.broadcast_to(b_spatial[:, :, None], (N_HEADS_A, CHUNK, HEAD_DIM_A))

    x1 = pl.pallas_call(
        _mixer_kernel,
        grid=(b, s // TS),
        in_specs=[pl.BlockSpec((1, TS, d), lambda i, j: (i, j, 0)),
                  pl.BlockSpec((1, N_MOD, d), lambda i, j: (i, 0, 0)),
                  _const_spec((1, d)), _const_spec((1, d)),
                  _const_spec(w_in.shape), _const_spec(w_spatial.shape), _const_spec(bsp.shape),
                  _const_spec((1, D_A)), _const_spec((1, D_A)),
                  _const_spec(w_pool.shape), _const_spec(b_pool.shape), _const_spec((1, D_B)),
                  _const_spec(w_out.shape)],
        out_specs=pl.BlockSpec((1, TS, d), lambda i, j: (i, j, 0)),
        out_shape=jax.ShapeDtypeStruct(x.shape, x.dtype),
        scratch_shapes=[pltpu.VMEM((HALO, D_B), jnp.float32),
                        pltpu.VMEM((2 * N_SLABS, PAD + HALO + SLAB, D_B), jnp.float32),
                        pltpu.VMEM((N_SLABS, SLAB, D_A + D_B), bf16)],
        compiler_params=pltpu.CompilerParams(
            dimension_semantics=("arbitrary", "arbitrary"), vmem_limit_bytes=VMEM_LIMIT_BYTES),
        name="token_mixer",
    )(x, mod, row(norm1_pre), row(norm1_post), w_in.astype(bf16), w_spatial, bsp,
      row(ln_v_gain), row(ln_v_bias), w_pool.astype(bf16), b_pool, row(pool_scale),
      w_out.astype(bf16))

    out = pl.pallas_call(
        _mlp_kernel,
        grid=(b, s // TM),
        in_specs=[pl.BlockSpec((1, TM, d), lambda i, j: (i, j, 0)),
                  pl.BlockSpec((1, N_MOD, d), lambda i, j: (i, 0, 0)),
                  _const_spec((1, d)), _const_spec((1, d)),
                  _const_spec(w_fc1.shape), _const_spec(w_fc2.shape)],
        out_specs=pl.BlockSpec((1, TM, d), lambda i, j: (i, j, 0)),
        out_shape=jax.ShapeDtypeStruct(x.shape, x.dtype),
        compiler_params=pltpu.CompilerParams(
            dimension_semantics=("arbitrary", "arbitrary"), vmem_limit_bytes=VMEM_LIMIT_BYTES),
        name="sqrelu_mlp",
    )(x1, mod, row(norm2_pre), row(norm2_post), w_fc1.astype(bf16), w_fc2.astype(bf16))
    return out
```

```python
import itertools
import math

import jax
import jax.numpy as jnp
from jax import lax
from jax.experimental import pallas as pl
from jax.experimental.pallas import tpu as pltpu

D_MODEL = 1024
D_A = 512
D_B = 512
N_HEADS_A = 4
HEAD_DIM_A = D_A // N_HEADS_A
CHUNK = 128
POOL_WINDOWS = (2, 4, 8, 16)
POOL_GROUP_DIM = D_B // len(POOL_WINDOWS)
D_FF = 4 * D_MODEL
N_MOD = 6
EPS = 1e-6

GELU_A = -2.0 * math.sqrt(2.0 / math.pi) * math.log2(math.e)
GELU_B = GELU_A * 0.044715

HALO = 16
PAD = 8

SLAB = 512
N_SLABS = 2
TS = SLAB * N_SLABS
TM = 1024
ADA_BN = 1536

VMEM_LIMIT_BYTES = 56 * 1024 * 1024


def _ada_kernel(c_ref, w_ref, b_ref, w_in_ref, w_out_ref, w_pool_ref,
                o_ref, w_in_o_ref, w_out_o_ref, w_pool_o_ref):
    c = c_ref[...]
    s = c * jax.nn.sigmoid(c)
    o_ref[...] = jnp.dot(s.astype(jnp.bfloat16), w_ref[...].astype(jnp.bfloat16),
                         preferred_element_type=jnp.float32) + b_ref[...]
    w_in_o_ref[:, :D_B] = w_in_ref[:, 2 * D_A:].astype(jnp.bfloat16)
    w_in_o_ref[:, D_B:D_B + D_A] = w_in_ref[:, D_A:2 * D_A].astype(jnp.bfloat16)
    w_in_o_ref[:, D_B + D_A:] = w_in_ref[:, :D_A].astype(jnp.bfloat16)
    w_out_o_ref[...] = w_out_ref[...].astype(jnp.bfloat16)
    w_pool_o_ref[...] = w_pool_ref[...].astype(jnp.bfloat16)


def _rms_scale(v):
    return lax.rsqrt(jnp.mean(v * v, axis=-1, keepdims=True) + EPS)


def _gelu_tanh(x):
    return x / (1.0 + jnp.exp2(x * (GELU_A + GELU_B * (x * x))))


def _pool_diffs(zb, halo, p_scr, q_scr, pos0):
    n = zb.shape[0]
    rows = HALO + n
    cur = PAD + HALO
    p_scr[0:PAD, :] = jnp.zeros((PAD, D_B), jnp.float32)
    q_scr[0:PAD, :] = jnp.zeros((PAD, D_B), jnp.float32)
    p_scr[PAD:cur, :] = halo
    p_scr[cur:cur + n, :] = zb
    src, dst = p_scr, q_scr
    shift = 1
    for level in range(len(POOL_WINDOWS)):
        lo = level * POOL_GROUP_DIM
        dst[PAD:PAD + rows, lo:] = (src[PAD:PAD + rows, lo:]
                                    + src[PAD - shift:PAD - shift + rows, lo:])
        src, dst = dst, src
        shift *= 2
    finals = [q_scr, p_scr, q_scr, p_scr]

    posf = (lax.broadcasted_iota(jnp.int32, (n, POOL_GROUP_DIM), 0) + (pos0 + 1)).astype(jnp.float32)
    diffs = []
    for g, w in enumerate(POOL_WINDOWS):
        lo = g * POOL_GROUP_DIM
        wsum = finals[g][cur:cur + n, lo:lo + POOL_GROUP_DIM]
        inv_count = 1.0 / jnp.minimum(posf, float(w))
        diffs.append(wsum * inv_count - zb[:, lo:lo + POOL_GROUP_DIM])
    return diffs


def _mix_slab(sl, halos, pos0, x_ref, o_ref, a, shift1, gpost, w_in_ref, w_causal, bsp_ref,
              lng_ref, lnb_ref, wpool_ref, bpool_ref, pscale_ref, w_out_ref, p_scr, q_scr, y_sl):
    n_chunks = SLAB // CHUNK
    r0 = sl * SLAB
    x = x_ref[0, r0:r0 + SLAB, :]
    h = (x * _rms_scale(x)) * a + shift1
    z = jnp.dot(h.astype(jnp.bfloat16), w_in_ref[...], preferred_element_type=jnp.float32)
    zb = z[:, :D_B]
    halos[sl + 1] = zb[SLAB - HALO:, :]
    yield

    diffs = _pool_diffs(zb, halos[sl], p_scr, q_scr, pos0 + r0)
    for g in range(len(POOL_WINDOWS)):
        lo = g * POOL_GROUP_DIM
        yb = jnp.dot(diffs[g].astype(jnp.bfloat16), wpool_ref[g],
                     preferred_element_type=jnp.float32)
        yb = (yb + bpool_ref[g:g + 1, :]) * pscale_ref[:, lo:lo + POOL_GROUP_DIM]
        y_sl[:, D_A + lo:D_A + lo + POOL_GROUP_DIM] = yb.astype(jnp.bfloat16)
    mix = jnp.dot(y_sl[:, D_A:], w_out_ref[D_A:, :], preferred_element_type=jnp.float32)

    v = _gelu_tanh(z[:, D_B:D_B + D_A])
    mu = jnp.mean(v, axis=-1, keepdims=True)
    vc = v - mu
    var = jnp.mean(vc * vc, axis=-1, keepdims=True)
    vn = ((vc * lax.rsqrt(var + EPS)) * lng_ref[...] + lnb_ref[...]).astype(jnp.bfloat16)
    mixed = []
    for hd in range(N_HEADS_A):
        c0 = hd * HEAD_DIM_A
        rhs = jnp.concatenate(
            [vn[c * CHUNK:(c + 1) * CHUNK, c0:c0 + HEAD_DIM_A] for c in range(n_chunks)], axis=1)
        mixed.append(jnp.dot(w_causal[hd], rhs, preferred_element_type=jnp.float32))
    yield

    u = _gelu_tanh(z[:, D_B + D_A:])
    for hd in range(N_HEADS_A):
        c0 = hd * HEAD_DIM_A
        bias = bsp_ref[hd]
        for c in range(n_chunks):
            m_c = mixed[hd][:, c * HEAD_DIM_A:(c + 1) * HEAD_DIM_A] + bias
            y_sl[c * CHUNK:(c + 1) * CHUNK, c0:c0 + HEAD_DIM_A] = (
                u[c * CHUNK:(c + 1) * CHUNK, c0:c0 + HEAD_DIM_A] * m_c).astype(jnp.bfloat16)
    mix = mix + jnp.dot(y_sl[:, :D_A], w_out_ref[:D_A, :], preferred_element_type=jnp.float32)
    yield

    o_ref[0, r0:r0 + SLAB, :] = x + gpost * (mix * _rms_scale(mix))


def _mixer_kernel(x_ref, mod_ref, g_pre_ref, g_post_ref, w_in_ref, wsp_ref, bsp_ref,
                  lng_ref, lnb_ref, wpool_ref, bpool_ref, pscale_ref, w_out_ref,
                  w1_ref, w2_ref, o_ref, w1_o_ref, w2_o_ref, carry_scr, pq_scr, y_scr):
    s_idx = pl.program_id(1)
    w1_o_ref[...] = w1_ref[...].astype(jnp.bfloat16)
    w2_o_ref[...] = w2_ref[...].astype(jnp.bfloat16)

    shift1 = mod_ref[0, 0:1, :]
    scale1 = mod_ref[0, 1:2, :]
    gate1 = mod_ref[0, 2:3, :]
    a = g_pre_ref[...] * (1.0 + scale1)
    gpost = gate1 * g_post_ref[...]

    t_i = lax.broadcasted_iota(jnp.int32, (CHUNK, CHUNK), 0)
    s_i = lax.broadcasted_iota(jnp.int32, (CHUNK, CHUNK), 1)
    causal = s_i <= t_i
    w_causal = [jnp.where(causal, wsp_ref[hd], 0.0).astype(jnp.bfloat16)
                for hd in range(N_HEADS_A)]

    @pl.when(s_idx == 0)
    def _():
        carry_scr[...] = jnp.zeros_like(carry_scr)

    halos = [carry_scr[...]] + [None] * N_SLABS
    slabs = [_mix_slab(sl, halos, s_idx * TS, x_ref, o_ref, a, shift1, gpost, w_in_ref, w_causal,
                       bsp_ref, lng_ref, lnb_ref, wpool_ref, bpool_ref, pscale_ref, w_out_ref,
                       pq_scr.at[2 * sl], pq_scr.at[2 * sl + 1], y_scr.at[sl])
             for sl in range(N_SLABS)]
    for _ in itertools.zip_longest(*slabs):
        pass
    carry_scr[...] = halos[N_SLABS]


def _mlp_kernel(x_ref, mod_ref, g_pre_ref, g_post_ref, w1_ref, w2_ref, o_ref):
    shift2 = mod_ref[0, 3:4, :]
    scale2 = mod_ref[0, 4:5, :]
    gate2 = mod_ref[0, 5:6, :]

    x = x_ref[0]
    a = g_pre_ref[...] * (1.0 + scale2)
    h = ((x * _rms_scale(x)) * a + shift2).astype(jnp.bfloat16)
    t = jnp.maximum(jnp.dot(h, w1_ref[...], preferred_element_type=jnp.float32), 0.0)
    p = (t * t).astype(jnp.bfloat16)
    f = jnp.dot(p, w2_ref[...], preferred_element_type=jnp.float32)
    o_ref[0] = x + (gate2 * g_post_ref[...]) * (f * _rms_scale(f))


def _const_spec(shape):
    nd = len(shape)
    return pl.BlockSpec(shape, lambda *_: (0,) * nd, pipeline_mode=pl.Buffered(1))


def kernel(x, c, w_ada, b_ada, norm1_pre, norm1_post, w_in, w_spatial, b_spatial, ln_v_gain,
           ln_v_bias, w_pool, b_pool, pool_scale, w_out, norm2_pre, norm2_post, w_fc1, w_fc2):
    b, s, d = x.shape
    bf16 = jnp.bfloat16
    row = lambda v: v.reshape(1, -1)
    n_pool = len(POOL_WINDOWS)

    n_ada = N_MOD * d // ADA_BN
    mod, w_in_b, w_out_b, w_pool_b = pl.pallas_call(
        _ada_kernel,
        grid=(n_ada,),
        in_specs=[pl.BlockSpec((b, d), lambda j: (0, 0)),
                  pl.BlockSpec((d, ADA_BN), lambda j: (0, j)),
                  pl.BlockSpec((1, ADA_BN), lambda j: (0, j)),
                  pl.BlockSpec((d // n_ada, w_in.shape[1]), lambda j: (j, 0)),
                  pl.BlockSpec((D_MODEL // n_ada, d), lambda j: (j, 0)),
                  pl.BlockSpec((n_pool // n_ada, POOL_GROUP_DIM, POOL_GROUP_DIM),
                               lambda j: (j, 0, 0))],
        out_specs=[pl.BlockSpec((b, ADA_BN), lambda j: (0, j)),
                   pl.BlockSpec((d // n_ada, w_in.shape[1]), lambda j: (j, 0)),
                   pl.BlockSpec((D_MODEL // n_ada, d), lambda j: (j, 0)),
                   pl.BlockSpec((n_pool // n_ada, POOL_GROUP_DIM, POOL_GROUP_DIM),
                                lambda j: (j, 0, 0))],
        out_shape=[jax.ShapeDtypeStruct((b, N_MOD * d), jnp.float32),
                   jax.ShapeDtypeStruct(w_in.shape, bf16),
                   jax.ShapeDtypeStruct(w_out.shape, bf16),
                   jax.ShapeDtypeStruct(w_pool.shape, bf16)],
        name="adaln_mod",
    )(c, w_ada, row(b_ada), w_in, w_out, w_pool)
    mod = mod.reshape(b, N_MOD, d)

    bsp = jnp.broadcast_to(b_spatial[:, :, None], (N_HEADS_A, CHUNK, HEAD_DIM_A))

    n_seq = s // TS
    n_steps = b * n_seq
    x1, w_fc1_b, w_fc2_b = pl.pallas_call(
        _mixer_kernel,
        grid=(b, n_seq),
        in_specs=[pl.BlockSpec((1, TS, d), lambda i, j: (i, j, 0)),
                  pl.BlockSpec((1, N_MOD, d), lambda i, j: (i, 0, 0)),
                  _const_spec((1, d)), _const_spec((1, d)),
                  _const_spec(w_in.shape), _const_spec(w_spatial.shape), _const_spec(bsp.shape),
                  _const_spec((1, D_A)), _const_spec((1, D_A)),
                  _const_spec(w_pool.shape), _const_spec(b_pool.shape), _const_spec((1, D_B)),
                  _const_spec(w_out.shape),
                  pl.BlockSpec((d // n_steps, D_FF), lambda i, j: (i * n_seq + j, 0)),
                  pl.BlockSpec((D_FF // n_steps, d), lambda i, j: (i * n_seq + j, 0))],
        out_specs=[pl.BlockSpec((1, TS, d), lambda i, j: (i, j, 0)),
                   pl.BlockSpec((d // n_steps, D_FF), lambda i, j: (i * n_seq + j, 0)),
                   pl.BlockSpec((D_FF // n_steps, d), lambda i, j: (i * n_seq + j, 0))],
        out_shape=[jax.ShapeDtypeStruct(x.shape, x.dtype),
                   jax.ShapeDtypeStruct(w_fc1.shape, bf16),
                   jax.ShapeDtypeStruct(w_fc2.shape, bf16)],
        scratch_shapes=[pltpu.VMEM((HALO, D_B), jnp.float32),
                        pltpu.VMEM((2 * N_SLABS, PAD + HALO + SLAB, D_B), jnp.float32),
                        pltpu.VMEM((N_SLABS, SLAB, D_A + D_B), bf16)],
        compiler_params=pltpu.CompilerParams(
            dimension_semantics=("arbitrary", "arbitrary"), vmem_limit_bytes=VMEM_LIMIT_BYTES),
        name="token_mixer",
    )(x, mod, row(norm1_pre), row(norm1_post), w_in_b, w_spatial, bsp,
      row(ln_v_gain), row(ln_v_bias), w_pool_b, b_pool, row(pool_scale), w_out_b, w_fc1, w_fc2)

    out = pl.pallas_call(
        _mlp_kernel,
        grid=(b, s // TM),
        in_specs=[pl.BlockSpec((1, TM, d), lambda i, j: (i, j, 0)),
                  pl.BlockSpec((1, N_MOD, d), lambda i, j: (i, 0, 0)),
                  _const_spec((1, d)), _const_spec((1, d)),
                  _const_spec(w_fc1.shape), _const_spec(w_fc2.shape)],
        out_specs=pl.BlockSpec((1, TM, d), lambda i, j: (i, j, 0)),
        out_shape=jax.ShapeDtypeStruct(x.shape, x.dtype),
        compiler_params=pltpu.CompilerParams(
            dimension_semantics=("arbitrary", "arbitrary"), vmem_limit_bytes=VMEM_LIMIT_BYTES),
        name="sqrelu_mlp",
    )(x1, mod, row(norm2_pre), row(norm2_post), w_fc1_b, w_fc2_b)
    return out
```
